```python
import math
import jax, jax.numpy as jnp
from jax import lax
import numpy as np

D_MODEL = 1024
BATCH = 16
SEQ = 4096
DEPTH = 2
DEC_BATCH = 8
DEC_SEQ = 4096
PAST_LEN = 128

N_MIXERS = 2
EXPAND = 2
D_INNER = EXPAND * D_MODEL
N_HEADS = 8
DA_HEAD_DIM = D_INNER // (2 * N_HEADS)
DA_V_DIM = 2 * DA_HEAD_DIM
DA_QK_WIDTH = 2 * N_HEADS * DA_HEAD_DIM
DA_IN = 2 * DA_QK_WIDTH + 2 * D_INNER
ROPE_THETA = 10000.0
Q_BLOCK = 128
RET_QK_DIM = D_MODEL // N_HEADS
RET_V_DIM = D_INNER // N_HEADS
RET_QK_WIDTH = N_HEADS * RET_QK_DIM
RET_IN = 2 * RET_QK_WIDTH + 2 * D_INNER
RET_CHUNK = 128
RET_ROT_BASE = 10000.0
N_ATTN_LAYERS = (DEPTH + 1) // 2
N_RET_LAYERS = DEPTH // 2
NORM_EPS = 1e-6
SUBLN_EPS = 1e-5

kernel_name = "diffattn_retnet_bidir_encoder"


def rms_norm(x, g, eps):
    xf = x.astype(jnp.float32)
    y = xf * lax.rsqrt(jnp.mean(xf * xf, axis=-1, keepdims=True) + eps)
    return (y * g.astype(jnp.float32)).astype(x.dtype)


def rotary_half(x):
    S, d = x.shape[1], x.shape[-1]
    inv = ROPE_THETA ** (-jnp.arange(0, d, 2, dtype=jnp.float32) / d)
    ang = jnp.arange(S, dtype=jnp.float32)[:, None] * inv[None, :]
    cos = jnp.cos(ang)[None, :, None, :].astype(x.dtype)
    sin = jnp.sin(ang)[None, :, None, :].astype(x.dtype)
    x1, x2 = x[..., : d // 2], x[..., d // 2:]
    return jnp.concatenate([x1 * cos - x2 * sin, x2 * cos + x1 * sin], axis=-1)


def retention_rotate(x):
    S, d = x.shape[1], x.shape[-1]
    angle = 1.0 / (RET_ROT_BASE ** jnp.linspace(0.0, 1.0, d // 2, dtype=jnp.float32))
    ang = jnp.arange(S, dtype=jnp.float32)[:, None] * angle[None, :]
    cos = jnp.cos(ang)[None, :, None, :].astype(x.dtype)
    sin = jnp.sin(ang)[None, :, None, :].astype(x.dtype)
    xp = x.reshape(*x.shape[:-1], d // 2, 2)
    x0, x1 = xp[..., 0], xp[..., 1]
    return jnp.stack([x0 * cos - x1 * sin, x1 * cos + x0 * sin], axis=-1).reshape(x.shape)


def diff_attention(h, w_in, lq1, lk1, lq2, lk2, subln_g, w_out, lambda_init):
    B, S, _ = h.shape
    proj = h @ w_in
    q, k, v, gate = jnp.split(proj, [DA_QK_WIDTH, 2 * DA_QK_WIDTH, 2 * DA_QK_WIDTH + D_INNER], axis=-1)
    q = rotary_half(q.reshape(B, S, 2 * N_HEADS, DA_HEAD_DIM)) * (DA_HEAD_DIM ** -0.5)
    k = rotary_half(k.reshape(B, S, 2 * N_HEADS, DA_HEAD_DIM))
    v = v.reshape(B, S, N_HEADS, DA_V_DIM)
    f32 = jnp.float32
    lam = (jnp.exp(jnp.sum(lq1.astype(f32) * lk1.astype(f32)))
           - jnp.exp(jnp.sum(lq2.astype(f32) * lk2.astype(f32))) + lambda_init)
    n_blk = S // Q_BLOCK
    q_blocks = q.reshape(B, n_blk, Q_BLOCK, 2 * N_HEADS, DA_HEAD_DIM).transpose(1, 0, 2, 3, 4)

    def block(qi):
        s = jnp.einsum('bqhd,bkhd->bhqk', qi, k).astype(f32)
        p = jax.nn.softmax(s, axis=-1).reshape(B, N_HEADS, 2, Q_BLOCK, S)
        a = (p[:, :, 0] - lam * p[:, :, 1]).astype(v.dtype)
        return jnp.einsum('bhqk,bkhe->bqhe', a, v)

    o = lax.map(block, q_blocks)
    o = o.transpose(1, 0, 2, 3, 4).reshape(B, S, N_HEADS, DA_V_DIM)
    o = rms_norm(o, subln_g, SUBLN_EPS) * (1.0 - lambda_init)
    o = o.reshape(B, S, D_INNER) * jax.nn.silu(gate)
    return o @ w_out


def retention_direction(q, k, v, log_g, include_diag):
    B, S, H, dk = q.shape
    dv = v.shape[-1]
    C = RET_CHUNK
    n_chunk = S // C
    dt = q.dtype
    idx = jnp.arange(C, dtype=jnp.float32)
    diff = idx[:, None] - idx[None, :]
    mask = (diff >= 0) if include_diag else (diff > 0)
    intra = jnp.where(mask[None], jnp.exp(jnp.where(mask, diff, 0.0)[None] * log_g[:, None, None]), 0.0).astype(dt)
    q_decay = jnp.exp((idx + 1.0)[:, None] * log_g[None, :]).astype(dt)
    k_decay = jnp.exp((C - 1.0 - idx)[:, None] * log_g[None, :]).astype(dt)
    chunk_decay = jnp.exp(C * log_g).astype(dt)

    def to_chunks(t):
        return t.reshape(B, n_chunk, C, H, t.shape[-1]).transpose(1, 0, 2, 3, 4)

    def step(state, inp):
        qc, kc, vc = inp
        scores = jnp.einsum('bqhd,bkhd->bhqk', qc, kc) * intra[None]
        o = (jnp.einsum('bhqk,bkhe->bqhe', scores, vc)
             + jnp.einsum('bqhd,bhde->bqhe', qc * q_decay[None, :, :, None], state))
        state = (state * chunk_decay[None, :, None, None]
                 + jnp.einsum('bkhd,bkhe->bhde', kc * k_decay[None, :, :, None], vc))
        return state, o

    state0 = jnp.zeros((B, H, dk, dv), dt)
    _, o = lax.scan(step, state0, (to_chunks(q), to_chunks(k), to_chunks(v)))
    return o.transpose(1, 0, 2, 3, 4).reshape(B, S, H, dv)


def bidir_retention(h, w_in, decay_fwd, decay_bwd, subln_g, w_out):
    B, S, _ = h.shape
    proj = h @ w_in
    q, k, v, gate = jnp.split(proj, [RET_QK_WIDTH, 2 * RET_QK_WIDTH, 2 * RET_QK_WIDTH + D_INNER], axis=-1)
    q = retention_rotate(q.reshape(B, S, N_HEADS, RET_QK_DIM))
    k = retention_rotate(k.reshape(B, S, N_HEADS, RET_QK_DIM)) * (RET_QK_DIM ** -0.5)
    v = v.reshape(B, S, N_HEADS, RET_V_DIM)
    log_g_f = -jnp.exp(decay_fwd.astype(jnp.float32))
    log_g_b = -jnp.exp(decay_bwd.astype(jnp.float32))
    o_f = retention_direction(q, k, v, log_g_f, True)
    o_b = jnp.flip(retention_direction(jnp.flip(q, 1), jnp.flip(k, 1), jnp.flip(v, 1), log_g_b, False), 1)
    o = rms_norm(o_f + o_b, subln_g, SUBLN_EPS)
    o = o.reshape(B, S, D_INNER) * jax.nn.silu(gate)
    return o @ w_out


def trunk(x, norm_g, da_w_in, da_lambda_q1, da_lambda_k1, da_lambda_q2, da_lambda_k2,
          da_subln_g, da_w_out, ret_w_in, ret_decay_fwd, ret_decay_bwd, ret_subln_g,
          ret_w_out, final_norm_g):
    for i in range(DEPTH):
        h = rms_norm(x, norm_g[i], NORM_EPS)
        j = i // N_MIXERS
        if i % N_MIXERS == 0:
            lambda_init = 0.8 - 0.6 * math.exp(-0.3 * i)
            x = x + diff_attention(h, da_w_in[j], da_lambda_q1[j], da_lambda_k1[j], da_lambda_q2[j],
                                   da_lambda_k2[j], da_subln_g[j], da_w_out[j], lambda_init)
        else:
            x = x + bidir_retention(h, ret_w_in[j], ret_decay_fwd[j], ret_decay_bwd[j],
                                    ret_subln_g[j], ret_w_out[j])
    return rms_norm(x, final_norm_g, NORM_EPS)


def setup_inputs(seed: int = 0) -> dict:
    key = jax.random.key(seed)
    ks = jax.random.split(key, 16)
    f32 = jnp.float32
    base_decay = jnp.log(-jnp.log1p(-(2.0 ** (-5.0 - jnp.arange(N_HEADS, dtype=f32)))))
    return {
        "x_prompt": jax.random.normal(ks[0], (BATCH, SEQ, D_MODEL), f32),
        "x_sample": jax.random.normal(ks[1], (DEC_BATCH, DEC_SEQ, D_MODEL), f32),
        "norm_g": 1.0 + 0.02 * jax.random.normal(ks[2], (DEPTH, D_MODEL), f32),
        "da_w_in": jax.random.normal(ks[3], (N_ATTN_LAYERS, D_MODEL, DA_IN), f32) * D_MODEL ** -0.5,
        "da_lambda_q1": 0.1 * jax.random.normal(ks[4], (N_ATTN_LAYERS, DA_HEAD_DIM), f32),
        "da_lambda_k1": 0.1 * jax.random.normal(ks[5], (N_ATTN_LAYERS, DA_HEAD_DIM), f32),
        "da_lambda_q2": 0.1 * jax.random.normal(ks[6], (N_ATTN_LAYERS, DA_HEAD_DIM), f32),
        "da_lambda_k2": 0.1 * jax.random.normal(ks[7], (N_ATTN_LAYERS, DA_HEAD_DIM), f32),
        "da_subln_g": 1.0 + 0.02 * jax.random.normal(ks[8], (N_ATTN_LAYERS, DA_V_DIM), f32),
        "da_w_out": jax.random.normal(ks[9], (N_ATTN_LAYERS, D_INNER, D_MODEL), f32) * D_INNER ** -0.5,
        "ret_w_in": jax.random.normal(ks[10], (N_RET_LAYERS, D_MODEL, RET_IN), f32) * D_MODEL ** -0.5,
        "ret_decay_fwd": base_decay[None] + 0.1 * jax.random.normal(ks[11], (N_RET_LAYERS, N_HEADS), f32),
        "ret_decay_bwd": base_decay[None] + 0.1 * jax.random.normal(ks[12], (N_RET_LAYERS, N_HEADS), f32),
        "ret_subln_g": 1.0 + 0.02 * jax.random.normal(ks[13], (N_RET_LAYERS, RET_V_DIM), f32),
        "ret_w_out": jax.random.normal(ks[14], (N_RET_LAYERS, D_INNER, D_MODEL), f32) * D_INNER ** -0.5,
        "final_norm_g": 1.0 + 0.02 * jax.random.normal(ks[15], (D_MODEL,), f32),
    }


def reference(x_prompt, x_sample, norm_g, da_w_in, da_lambda_q1, da_lambda_k1, da_lambda_q2,
              da_lambda_k2, da_subln_g, da_w_out, ret_w_in, ret_decay_fwd, ret_decay_bwd,
              ret_subln_g, ret_w_out, final_norm_g):
    y_prompt = trunk(x_prompt, norm_g, da_w_in, da_lambda_q1, da_lambda_k1, da_lambda_q2, da_lambda_k2,
                     da_subln_g, da_w_out, ret_w_in, ret_decay_fwd, ret_decay_bwd, ret_subln_g,
                     ret_w_out, final_norm_g)
    y_sample = trunk(x_sample, norm_g, da_w_in, da_lambda_q1, da_lambda_k1, da_lambda_q2, da_lambda_k2,
                     da_subln_g, da_w_out, ret_w_in, ret_decay_fwd, ret_decay_bwd, ret_subln_g,
                     ret_w_out, final_norm_g)
    return (y_prompt, y_sample)
```

```python
import functools
import math

import jax
import jax.numpy as jnp
from jax import lax
from jax.experimental import pallas as pl
from jax.experimental.pallas import tpu as pltpu

F32 = jnp.float32
BF16 = jnp.bfloat16

D_MODEL = 1024
D_INNER = 2048
N_HEADS = 8
QK_DIM = 128
V_DIM = 256
ROPE_THETA = 10000.0
RET_ROT_BASE = 10000.0
NORM_EPS = 1e-6
SUBLN_EPS = 1e-5
RET_CHUNK = 128

VMEM_LIMIT_BYTES = 56 * 1024 * 1024

PROJ_TM = 1024
PROJ_TN = 512
OUT_TM = 512
ATTN_TQ = 128


def _nt_dot(a, b):
    return lax.dot_general(a, b, (((1,), (1,)), ((), ())), preferred_element_type=F32)


def _tn_dot(a, b):
    return lax.dot_general(a, b, (((0,), (0,)), ((), ())), preferred_element_type=F32)


def _in_proj_kernel(x_ref, g_ref, w_ref, cq_ref, sq_ref, ck_ref, sk_ref, o_ref, h_scr,
                    *, nq, nk, nv, tn):
    j = pl.program_id(1)

    @pl.when(j == 0)
    def _():
        x = x_ref[...]
        ms = jnp.mean(x * x, axis=-1, keepdims=True)
        h_scr[...] = (x * lax.rsqrt(ms + NORM_EPS) * g_ref[...]).astype(BF16)

    res = jnp.dot(h_scr[...], w_ref[...], preferred_element_type=F32)

    def store(vals):
        for c in range(tn // V_DIM):
            o_ref[c] = vals[c].astype(BF16)

    def split(vals):
        return [vals[:, c * V_DIM:(c + 1) * V_DIM] for c in range(tn // V_DIM)]

    def rotate(cos_ref, sin_ref):
        cos = cos_ref[...]
        sin = sin_ref[...]
        outs = []
        for c in range(tn // V_DIM):
            halves = []
            for t in range(V_DIM // QK_DIM):
                lo = c * V_DIM + t * QK_DIM
                xh = res[:, lo:lo + QK_DIM]
                halves.append(xh * cos + pltpu.roll(xh, QK_DIM // 2, 1) * sin)
            outs.append(jnp.concatenate(halves, axis=1))
        return outs

    @pl.when(j < nq)
    def _():
        store(rotate(cq_ref, sq_ref))

    @pl.when((j >= nq) & (j < nq + nk))
    def _():
        store(rotate(ck_ref, sk_ref))

    @pl.when((j >= nq + nk) & (j < nq + nk + nv))
    def _():
        store(split(res))

    @pl.when(j >= nq + nk + nv)
    def _():
        store(split(res * jax.nn.sigmoid(res)))


def _in_proj(x2d, g, w, tabs, *, seq, q_width, k_width, v_width):
    T, D = x2d.shape
    N = w.shape[1]
    tm, tn = min(PROJ_TM, seq), PROJ_TN
    assert T % tm == 0 and seq % tm == 0 and N % tn == 0
    assert q_width % tn == 0 and k_width % tn == 0 and v_width % tn == 0
    spb = seq // tm
    tab_spec = pl.BlockSpec((tm, QK_DIM), lambda i, j: (i % spb, 0))
    kern = functools.partial(_in_proj_kernel, nq=q_width // tn, nk=k_width // tn,
                             nv=v_width // tn, tn=tn)
    return pl.pallas_call(
        kern,
        grid=(T // tm, N // tn),
        in_specs=[
            pl.BlockSpec((tm, D), lambda i, j: (i, 0)),
            pl.BlockSpec((1, D), lambda i, j: (0, 0)),
            pl.BlockSpec((D, tn), lambda i, j: (0, j)),
            tab_spec, tab_spec, tab_spec, tab_spec,
        ],
        out_specs=pl.BlockSpec((tn // V_DIM, tm, V_DIM), lambda i, j: (j, i, 0)),
        out_shape=jax.ShapeDtypeStruct((N // V_DIM, T, V_DIM), BF16),
        scratch_shapes=[pltpu.VMEM((tm, D), BF16)],
        compiler_params=pltpu.CompilerParams(
            dimension_semantics=("arbitrary", "arbitrary"), vmem_limit_bytes=VMEM_LIMIT_BYTES),
        name="in_proj",
    )(x2d, g.reshape(1, D), w, *tabs)


def _subln_gate(o, g_ref, gate, scale):
    ms = jnp.mean(o * o, axis=-1, keepdims=True)
    o = o * lax.rsqrt(ms + SUBLN_EPS) * g_ref[...]
    if scale != 1.0:
        o = o * scale
    return (o * gate.astype(F32)).astype(BF16)


def _diff_attn_kernel(q_ref, k_ref, v_ref, gate_ref, lamv_ref, g_ref, o_ref, *, tq, lambda_init):
    seq = q_ref.shape[1]
    lv = lamv_ref[...]
    lam = (jnp.exp(jnp.sum(lv[0:1] * lv[1:2], axis=-1, keepdims=True))
           - jnp.exp(jnp.sum(lv[2:3] * lv[3:4], axis=-1, keepdims=True)) + lambda_init)

    def body(qi, carry):
        r0 = pl.multiple_of(qi * tq, tq)
        q = q_ref[0, pl.ds(r0, tq), :]
        s1 = _nt_dot(q[:, :QK_DIM], k_ref[0, :, :QK_DIM])
        s2 = _nt_dot(q[:, QK_DIM:], k_ref[0, :, QK_DIM:])
        p1 = jnp.exp(s1 - jnp.max(s1, axis=-1, keepdims=True))
        p2 = jnp.exp(s2 - jnp.max(s2, axis=-1, keepdims=True))
        r1 = 1.0 / jnp.sum(p1, axis=-1, keepdims=True)
        r2 = lam / jnp.sum(p2, axis=-1, keepdims=True)
        a = (p1 * r1 - p2 * r2).astype(BF16)
        o = jnp.dot(a, v_ref[0], preferred_element_type=F32)
        o_ref[pl.ds(r0, tq), :] = _subln_gate(o, g_ref, gate_ref[0, pl.ds(r0, tq), :],
                                              1.0 - lambda_init)
        return carry

    lax.fori_loop(0, seq // tq, body, 0)


def _diff_attn(proj, lamv, subln_g, *, batch, seq, lambda_init):
    T = proj.shape[1]
    tq = min(ATTN_TQ, seq)

    def blk(off):
        return pl.BlockSpec((1, seq, V_DIM), lambda b, h: (off + h, b, 0))

    kern = functools.partial(_diff_attn_kernel, tq=tq, lambda_init=lambda_init)
    return pl.pallas_call(
        kern,
        grid=(batch, N_HEADS),
        in_specs=[blk(0), blk(N_HEADS), blk(2 * N_HEADS), blk(3 * N_HEADS),
                  pl.BlockSpec((4, QK_DIM), lambda b, h: (0, 0)),
                  pl.BlockSpec((1, V_DIM), lambda b, h: (0, 0))],
        out_specs=pl.BlockSpec((seq, V_DIM), lambda b, h: (b, h)),
        out_shape=jax.ShapeDtypeStruct((T, D_INNER), BF16),
        compiler_params=pltpu.CompilerParams(
            dimension_semantics=("arbitrary", "arbitrary"), vmem_limit_bytes=VMEM_LIMIT_BYTES),
        name="diff_attn",
    )(proj, proj, proj, proj, lamv, subln_g.reshape(1, V_DIM))


def _retention_kernel(q_ref, k_ref, v_ref, gate_ref, df_ref, db_ref, g_ref, o_ref, sb_scr):
    C = RET_CHUNK
    seq = q_ref.shape[1]
    n_chunk = seq // C
    lgf = -jnp.exp(df_ref[0])
    lgb = -jnp.exp(db_ref[0])
    row = lax.broadcasted_iota(jnp.int32, (C, C), 0).astype(F32)
    col = lax.broadcasted_iota(jnp.int32, (C, C), 1).astype(F32)
    diff = row - col
    dmask = jnp.where(diff >= 0, jnp.exp(jnp.maximum(diff, 0.0) * lgf),
                      jnp.exp(jnp.maximum(-diff, 0.0) * lgb))
    qdec_f = jnp.exp((row + 1.0) * lgf)
    kdec_f = jnp.exp((C - 1.0 - row) * lgf)
    qdec_b = jnp.exp((C - row) * lgb)
    kdec_b = jnp.exp(row * lgb)
    cdec_f = jnp.exp(C * lgf)
    cdec_b = jnp.exp(C * lgb)
    cdec_f2 = jnp.concatenate([cdec_f, cdec_f], axis=1)
    cdec_b2 = jnp.concatenate([cdec_b, cdec_b], axis=1)

    def bwd_body(t, state):
        c = n_chunk - 1 - t
        r0 = pl.multiple_of(c * C, C)
        sb_scr[c] = state.astype(BF16)
        kc = (k_ref[0, pl.ds(r0, C), :].astype(F32) * kdec_b).astype(BF16)
        return state * cdec_b2 + _tn_dot(kc, v_ref[0, pl.ds(r0, C), :])

    lax.fori_loop(0, n_chunk, bwd_body, jnp.zeros((QK_DIM, V_DIM), F32))

    def fwd_body(c, state):
        r0 = pl.multiple_of(c * C, C)
        qc = q_ref[0, pl.ds(r0, C), :]
        kc = k_ref[0, pl.ds(r0, C), :]
        vc = v_ref[0, pl.ds(r0, C), :]
        qf = qc.astype(F32)
        scores = (_nt_dot(qc, kc) * dmask).astype(BF16)
        o = (jnp.dot(scores, vc, preferred_element_type=F32)
             + jnp.dot((qf * qdec_f).astype(BF16), state.astype(BF16), preferred_element_type=F32)
             + jnp.dot((qf * qdec_b).astype(BF16), sb_scr[c], preferred_element_type=F32))
        o_ref[pl.ds(r0, C), :] = _subln_gate(o, g_ref, gate_ref[0, pl.ds(r0, C), :], 1.0)
        kd = (kc.astype(F32) * kdec_f).astype(BF16)
        return state * cdec_f2 + _tn_dot(kd, vc)

    lax.fori_loop(0, n_chunk, fwd_body, jnp.zeros((QK_DIM, V_DIM), F32))


def _retention(proj, decay_f, decay_b, subln_g, *, batch, seq):
    T = proj.shape[1]
    nqb = N_HEADS * QK_DIM // V_DIM

    def qk_blk(off):
        return pl.BlockSpec((1, seq, QK_DIM), lambda b, h: (off + h // 2, b, h % 2))

    def blk(off):
        return pl.BlockSpec((1, seq, V_DIM), lambda b, h: (off + h, b, 0))

    dec_spec = pl.BlockSpec((1, 1, QK_DIM), lambda b, h: (h, 0, 0))
    bcast = lambda d: jnp.broadcast_to(d.astype(F32)[:, None, None], (N_HEADS, 1, QK_DIM))
    return pl.pallas_call(
        _retention_kernel,
        grid=(batch, N_HEADS),
        in_specs=[qk_blk(0), qk_blk(nqb), blk(2 * nqb), blk(2 * nqb + N_HEADS),
                  dec_spec, dec_spec, pl.BlockSpec((1, V_DIM), lambda b, h: (0, 0))],
        out_specs=pl.BlockSpec((seq, V_DIM), lambda b, h: (b, h)),
        out_shape=jax.ShapeDtypeStruct((T, D_INNER), BF16),
        scratch_shapes=[pltpu.VMEM((seq // RET_CHUNK, QK_DIM, V_DIM), BF16)],
        compiler_params=pltpu.CompilerParams(
            dimension_semantics=("arbitrary", "arbitrary"), vmem_limit_bytes=VMEM_LIMIT_BYTES),
        name="retention",
    )(proj, proj, proj, proj, bcast(decay_f), bcast(decay_b), subln_g.reshape(1, V_DIM))


def _out_proj_kernel(o_ref, w_ref, x_ref, g_ref, y_ref, *, final_norm):
    y = x_ref[...] + jnp.dot(o_ref[...], w_ref[...], preferred_element_type=F32)
    if final_norm:
        ms = jnp.mean(y * y, axis=-1, keepdims=True)
        y = y * lax.rsqrt(ms + NORM_EPS) * g_ref[...]
    y_ref[...] = y


def _out_proj(o, w, x2d, g, *, final_norm):
    T, D = x2d.shape
    tm = min(OUT_TM, T)
    assert T % tm == 0
    kern = functools.partial(_out_proj_kernel, final_norm=final_norm)
    return pl.pallas_call(
        kern,
        grid=(T // tm,),
        in_specs=[pl.BlockSpec((tm, D_INNER), lambda i: (i, 0)),
                  pl.BlockSpec((D_INNER, D), lambda i: (0, 0)),
                  pl.BlockSpec((tm, D), lambda i: (i, 0)),
                  pl.BlockSpec((1, D), lambda i: (0, 0))],
        out_specs=pl.BlockSpec((tm, D), lambda i: (i, 0)),
        out_shape=jax.ShapeDtypeStruct((T, D), F32),
        compiler_params=pltpu.CompilerParams(
            dimension_semantics=("arbitrary",), vmem_limit_bytes=VMEM_LIMIT_BYTES),
        name="out_proj",
    )(o, w, x2d, g.reshape(1, D))


def _rot_tables(seq, inv_freq, scale):
    ang = jnp.arange(seq, dtype=F32)[:, None] * inv_freq[None, :]
    cos, sin = jnp.cos(ang), jnp.sin(ang)
    return (jnp.concatenate([cos, cos], axis=1) * scale,
            jnp.concatenate([-sin, sin], axis=1) * scale)


def _trunk(x, p, *, depth):
    batch, seq, _ = x.shape
    x2d = x.reshape(batch * seq, D_MODEL)
    half = QK_DIM // 2
    da_inv = ROPE_THETA ** (-jnp.arange(0, QK_DIM, 2, dtype=F32) / QK_DIM)
    ret_inv = 1.0 / (RET_ROT_BASE ** jnp.linspace(0.0, 1.0, half, dtype=F32))
    qk_scale = QK_DIM ** -0.5
    da_tabs = _rot_tables(seq, da_inv, qk_scale) + _rot_tables(seq, da_inv, 1.0)
    ret_tabs = _rot_tables(seq, ret_inv, 1.0) + _rot_tables(seq, ret_inv, qk_scale)
    for i in range(depth):
        j = i // 2
        last = i == depth - 1
        if i % 2 == 0:
            lambda_init = 0.8 - 0.6 * math.exp(-0.3 * i)
            proj = _in_proj(x2d, p["norm_g"][i], p["da_w_in"][j], da_tabs, seq=seq,
                            q_width=2 * N_HEADS * QK_DIM, k_width=2 * N_HEADS * QK_DIM,
                            v_width=D_INNER)
            lamv = jnp.stack([p["da_lambda_q1"][j], p["da_lambda_k1"][j],
                              p["da_lambda_q2"][j], p["da_lambda_k2"][j]]).astype(F32)
            o = _diff_attn(proj, lamv, p["da_subln_g"][j], batch=batch, seq=seq,
                           lambda_init=lambda_init)
            w_out = p["da_w_out"][j]
        else:
            proj = _in_proj(x2d, p["norm_g"][i], p["ret_w_in"][j], ret_tabs, seq=seq,
                            q_width=N_HEADS * QK_DIM, k_width=N_HEADS * QK_DIM, v_width=D_INNER)
            o = _retention(proj, p["ret_decay_fwd"][j], p["ret_decay_bwd"][j],
                           p["ret_subln_g"][j], batch=batch, seq=seq)
            w_out = p["ret_w_out"][j]
        x2d = _out_proj(o, w_out, x2d, p["final_norm_g"], final_norm=last)
    return x2d.reshape(batch, seq, D_MODEL)


def _prep_params(norm_g, da_w_in, da_lambda_q1, da_lambda_k1, da_lambda_q2, da_lambda_k2,
                 da_subln_g, da_w_out, ret_w_in, ret_decay_fwd, ret_decay_bwd, ret_subln_g,
                 ret_w_out, final_norm_g):
    perm_head = jnp.concatenate([jnp.arange(0, QK_DIM, 2), jnp.arange(1, QK_DIM, 2)])
    qk_cols = 2 * N_HEADS * QK_DIM
    perm = (jnp.arange(qk_cols // QK_DIM)[:, None] * QK_DIM + perm_head[None, :]).reshape(-1)
    ret_w = jnp.concatenate([ret_w_in[:, :, :qk_cols][:, :, perm], ret_w_in[:, :, qk_cols:]], axis=2)
    return dict(
        norm_g=norm_g.astype(F32), final_norm_g=final_norm_g.astype(F32),
        da_w_in=da_w_in.astype(BF16), da_w_out=da_w_out.astype(BF16),
        da_lambda_q1=da_lambda_q1, da_lambda_k1=da_lambda_k1,
        da_lambda_q2=da_lambda_q2, da_lambda_k2=da_lambda_k2,
        da_subln_g=da_subln_g.astype(F32),
        ret_w_in=ret_w.astype(BF16), ret_w_out=ret_w_out.astype(BF16),
        ret_decay_fwd=ret_decay_fwd, ret_decay_bwd=ret_decay_bwd,
        ret_subln_g=ret_subln_g.astype(F32),
    )


def kernel(x_prompt, x_sample, norm_g, da_w_in, da_lambda_q1, da_lambda_k1, da_lambda_q2,
           da_lambda_k2, da_subln_g, da_w_out, ret_w_in, ret_decay_fwd, ret_decay_bwd,
           ret_subln_g, ret_w_out, final_norm_g):
    p = _prep_params(norm_g, da_w_in, da_lambda_q1, da_lambda_k1, da_lambda_q2, da_lambda_k2,
                     da_subln_g, da_w_out, ret_w_in, ret_decay_fwd, ret_decay_bwd, ret_subln_g,
                     ret_w_out, final_norm_g)
    depth = norm_g.shape[0]
    return (_trunk(x_prompt, p, depth=depth), _trunk(x_sample, p, depth=depth))
```

```python
import functools
import math

import jax
import jax.numpy as jnp
from jax import lax
from jax.experimental import pallas as pl
from jax.experimental.pallas import tpu as pltpu

F32 = jnp.float32
BF16 = jnp.bfloat16

D_MODEL = 1024
D_INNER = 2048
N_HEADS = 8
QK_DIM = 128
V_DIM = 256
ROPE_THETA = 10000.0
RET_ROT_BASE = 10000.0
NORM_EPS = 1e-6
SUBLN_EPS = 1e-5
RET_CHUNK = 128

VMEM_LIMIT_BYTES = 56 * 1024 * 1024

PROJ_TM = 1024
PROJ_TN = 512
OUT_TM = 512
ATTN_TQ = 256
ATTN_CK = 512


def _nt_dot(a, b):
    return lax.dot_general(a, b, (((1,), (1,)), ((), ())), preferred_element_type=F32)


def _tn_dot(a, b):
    return lax.dot_general(a, b, (((0,), (0,)), ((), ())), preferred_element_type=F32)


def _in_proj_kernel(x_ref, g_ref, w_ref, cq_ref, sq_ref, ck_ref, sk_ref, o_ref, h_scr,
                    *, nq, nk, nv, tn):
    j = pl.program_id(1)

    @pl.when(j == 0)
    def _():
        x = x_ref[...]
        ms = jnp.mean(x * x, axis=-1, keepdims=True)
        h_scr[...] = (x * lax.rsqrt(ms + NORM_EPS) * g_ref[...]).astype(BF16)

    res = jnp.dot(h_scr[...], w_ref[...], preferred_element_type=F32)

    def store(vals):
        for c in range(tn // V_DIM):
            o_ref[c] = vals[c].astype(BF16)

    def split(vals):
        return [vals[:, c * V_DIM:(c + 1) * V_DIM] for c in range(tn // V_DIM)]

    def rotate(cos_ref, sin_ref):
        cos = cos_ref[...]
        sin = sin_ref[...]
        outs = []
        for c in range(tn // V_DIM):
            halves = []
            for t in range(V_DIM // QK_DIM):
                lo = c * V_DIM + t * QK_DIM
                xh = res[:, lo:lo + QK_DIM]
                halves.append(xh * cos + pltpu.roll(xh, QK_DIM // 2, 1) * sin)
            outs.append(jnp.concatenate(halves, axis=1))
        return outs

    @pl.when(j < nq)
    def _():
        store(rotate(cq_ref, sq_ref))

    @pl.when((j >= nq) & (j < nq + nk))
    def _():
        store(rotate(ck_ref, sk_ref))

    @pl.when((j >= nq + nk) & (j < nq + nk + nv))
    def _():
        store(split(res))

    @pl.when(j >= nq + nk + nv)
    def _():
        store(split(res * jax.nn.sigmoid(res)))


def _in_proj(x2d, g, w, tabs, *, seq, q_width, k_width, v_width):
    T, D = x2d.shape
    N = w.shape[1]
    tm, tn = min(PROJ_TM, seq), PROJ_TN
    assert T % tm == 0 and seq % tm == 0 and N % tn == 0
    assert q_width % tn == 0 and k_width % tn == 0 and v_width % tn == 0
    spb = seq // tm
    tab_spec = pl.BlockSpec((tm, QK_DIM), lambda i, j: (i % spb, 0))
    kern = functools.partial(_in_proj_kernel, nq=q_width // tn, nk=k_width // tn,
                             nv=v_width // tn, tn=tn)
    return pl.pallas_call(
        kern,
        grid=(T // tm, N // tn),
        in_specs=[
            pl.BlockSpec((tm, D), lambda i, j: (i, 0)),
            pl.BlockSpec((1, D), lambda i, j: (0, 0)),
            pl.BlockSpec((D, tn), lambda i, j: (0, j)),
            tab_spec, tab_spec, tab_spec, tab_spec,
        ],
        out_specs=pl.BlockSpec((tn // V_DIM, tm, V_DIM), lambda i, j: (j, i, 0)),
        out_shape=jax.ShapeDtypeStruct((N // V_DIM, T, V_DIM), BF16),
        scratch_shapes=[pltpu.VMEM((tm, D), BF16)],
        compiler_params=pltpu.CompilerParams(
            dimension_semantics=("arbitrary", "arbitrary"), vmem_limit_bytes=VMEM_LIMIT_BYTES),
        name="in_proj",
    )(x2d, g.reshape(1, D), w, *tabs)


def _subln_gate(o, g_ref, gate, scale):
    ms = jnp.mean(o * o, axis=-1, keepdims=True)
    o = o * lax.rsqrt(ms + SUBLN_EPS) * g_ref[...]
    if scale != 1.0:
        o = o * scale
    return (o * gate.astype(F32)).astype(BF16)


def _diff_attn_kernel(q_ref, k_ref, v_ref, gate_ref, lamv_ref, g_ref, o_ref,
                      sa_scr, sb_scr, ma_scr, mb_scr, *, tq, ck, lambda_init):
    seq = q_ref.shape[1]
    n_tiles = seq // tq
    lanes = QK_DIM
    buf_a, buf_b = (sa_scr, ma_scr), (sb_scr, mb_scr)
    lv = lamv_ref[...]
    lam = (jnp.exp(jnp.sum(lv[0:1] * lv[1:2], axis=-1, keepdims=True))
           - jnp.exp(jnp.sum(lv[2:3] * lv[3:4], axis=-1, keepdims=True)) + lambda_init)

    def scores(t, bufs):
        s_scr, m_scr = bufs
        r0 = pl.multiple_of(t * tq, tq)
        q = q_ref[0, pl.ds(r0, tq), :]
        for mp in range(2):
            qm = q[:, mp * QK_DIM:(mp + 1) * QK_DIM]
            macc = None
            for kc in range(seq // ck):
                s = _nt_dot(qm, k_ref[0, kc * ck:(kc + 1) * ck, mp * QK_DIM:(mp + 1) * QK_DIM])
                s_scr[mp, :, kc * ck:(kc + 1) * ck] = s
                for g in range(ck // lanes):
                    sg = s[:, g * lanes:(g + 1) * lanes]
                    macc = sg if macc is None else jnp.maximum(macc, sg)
            m_scr[mp] = jnp.broadcast_to(jnp.max(macc, axis=-1, keepdims=True), (tq, lanes))

    def finish(t, bufs):
        s_scr, m_scr = bufs
        r0 = pl.multiple_of(t * tq, tq)
        sums = []
        for mp in range(2):
            m = m_scr[mp]
            lacc = jnp.zeros((tq, lanes), F32)
            for g in range(seq // lanes):
                sl = slice(g * lanes, (g + 1) * lanes)
                p = jnp.exp2(s_scr[mp, :, sl] - m)
                s_scr[mp, :, sl] = p
                lacc = lacc + p
            sums.append(jnp.sum(lacc, axis=-1, keepdims=True))
        r1 = 1.0 / sums[0]
        c = jnp.broadcast_to(lam * sums[0] / sums[1], (tq, lanes))
        acc = jnp.zeros((tq, V_DIM), F32)
        for kc in range(seq // ck):
            parts = []
            for g in range(ck // lanes):
                sl = slice(kc * ck + g * lanes, kc * ck + (g + 1) * lanes)
                parts.append((s_scr[0, :, sl] - c * s_scr[1, :, sl]).astype(BF16))
            a = jnp.concatenate(parts, axis=1)
            acc = acc + jnp.dot(a, v_ref[0, kc * ck:(kc + 1) * ck, :], preferred_element_type=F32)
        o_ref[pl.ds(r0, tq), :] = _subln_gate(acc * r1, g_ref, gate_ref[0, pl.ds(r0, tq), :],
                                              1.0 - lambda_init)

    scores(0, buf_a)

    def body(j, carry):
        t0 = 2 * j
        scores(t0 + 1, buf_b)
        finish(t0, buf_a)
        scores(t0 + 2, buf_a)
        finish(t0 + 1, buf_b)
        return carry

    lax.fori_loop(0, n_tiles // 2 - 1, body, 0)
    scores(n_tiles - 1, buf_b)
    finish(n_tiles - 2, buf_a)
    finish(n_tiles - 1, buf_b)


def _diff_attn(proj, lamv, subln_g, *, batch, seq, lambda_init):
    T = proj.shape[1]
    tq, ck = min(ATTN_TQ, seq), min(ATTN_CK, seq)
    assert seq % (2 * tq) == 0 and seq % ck == 0

    def blk(off):
        return pl.BlockSpec((1, seq, V_DIM), lambda b, h: (off + h, b, 0))

    kern = functools.partial(_diff_attn_kernel, tq=tq, ck=ck, lambda_init=lambda_init)
    return pl.pallas_call(
        kern,
        grid=(batch, N_HEADS),
        in_specs=[blk(0), blk(N_HEADS), blk(2 * N_HEADS), blk(3 * N_HEADS),
                  pl.BlockSpec((4, QK_DIM), lambda b, h: (0, 0)),
                  pl.BlockSpec((1, V_DIM), lambda b, h: (0, 0))],
        out_specs=pl.BlockSpec((seq, V_DIM), lambda b, h: (b, h)),
        out_shape=jax.ShapeDtypeStruct((T, D_INNER), BF16),
        scratch_shapes=[pltpu.VMEM((2, tq, seq), F32), pltpu.VMEM((2, tq, seq), F32),
                        pltpu.VMEM((2, tq, QK_DIM), F32), pltpu.VMEM((2, tq, QK_DIM), F32)],
        compiler_params=pltpu.CompilerParams(
            dimension_semantics=("arbitrary", "arbitrary"), vmem_limit_bytes=VMEM_LIMIT_BYTES),
        name="diff_attn",
    )(proj, proj, proj, proj, lamv, subln_g.reshape(1, V_DIM))


def _retention_kernel(q_ref, k_ref, v_ref, gate_ref, df_ref, db_ref, g_ref, o_ref, sb_scr):
    C = RET_CHUNK
    seq = q_ref.shape[1]
    n_chunk = seq // C
    lgf = -jnp.exp(df_ref[0])
    lgb = -jnp.exp(db_ref[0])
    row = lax.broadcasted_iota(jnp.int32, (C, C), 0).astype(F32)
    col = lax.broadcasted_iota(jnp.int32, (C, C), 1).astype(F32)
    diff = row - col
    dmask = jnp.where(diff >= 0, jnp.exp(jnp.maximum(diff, 0.0) * lgf),
                      jnp.exp(jnp.maximum(-diff, 0.0) * lgb))
    qdec_f = jnp.exp((row + 1.0) * lgf)
    kdec_f = jnp.exp((C - 1.0 - row) * lgf)
    qdec_b = jnp.exp((C - row) * lgb)
    kdec_b = jnp.exp(row * lgb)
    cdec_f = jnp.exp(C * lgf)
    cdec_b = jnp.exp(C * lgb)
    cdec_f2 = jnp.concatenate([cdec_f, cdec_f], axis=1)
    cdec_b2 = jnp.concatenate([cdec_b, cdec_b], axis=1)

    def bwd_body(t, state):
        c = n_chunk - 1 - t
        r0 = pl.multiple_of(c * C, C)
        sb_scr[c] = state.astype(BF16)
        kc = (k_ref[0, pl.ds(r0, C), :].astype(F32) * kdec_b).astype(BF16)
        return state * cdec_b2 + _tn_dot(kc, v_ref[0, pl.ds(r0, C), :])

    lax.fori_loop(0, n_chunk, bwd_body, jnp.zeros((QK_DIM, V_DIM), F32))

    def fwd_body(c, state):
        r0 = pl.multiple_of(c * C, C)
        qc = q_ref[0, pl.ds(r0, C), :]
        kc = k_ref[0, pl.ds(r0, C), :]
        vc = v_ref[0, pl.ds(r0, C), :]
        qf = qc.astype(F32)
        scores = (_nt_dot(qc, kc) * dmask).astype(BF16)
        o = (jnp.dot(scores, vc, preferred_element_type=F32)
             + jnp.dot((qf * qdec_f).astype(BF16), state.astype(BF16), preferred_element_type=F32)
             + jnp.dot((qf * qdec_b).astype(BF16), sb_scr[c], preferred_element_type=F32))
        o_ref[pl.ds(r0, C), :] = _subln_gate(o, g_ref, gate_ref[0, pl.ds(r0, C), :], 1.0)
        kd = (kc.astype(F32) * kdec_f).astype(BF16)
        return state * cdec_f2 + _tn_dot(kd, vc)

    lax.fori_loop(0, n_chunk, fwd_body, jnp.zeros((QK_DIM, V_DIM), F32))


def _retention(proj, decay_f, decay_b, subln_g, *, batch, seq):
    T = proj.shape[1]
    nqb = N_HEADS * QK_DIM // V_DIM

    def qk_blk(off):
        return pl.BlockSpec((1, seq, QK_DIM), lambda b, h: (off + h // 2, b, h % 2))

    def blk(off):
        return pl.BlockSpec((1, seq, V_DIM), lambda b, h: (off + h, b, 0))

    dec_spec = pl.BlockSpec((1, 1, QK_DIM), lambda b, h: (h, 0, 0))
    bcast = lambda d: jnp.broadcast_to(d.astype(F32)[:, None, None], (N_HEADS, 1, QK_DIM))
    return pl.pallas_call(
        _retention_kernel,
        grid=(batch, N_HEADS),
        in_specs=[qk_blk(0), qk_blk(nqb), blk(2 * nqb), blk(2 * nqb + N_HEADS),
                  dec_spec, dec_spec, pl.BlockSpec((1, V_DIM), lambda b, h: (0, 0))],
        out_specs=pl.BlockSpec((seq, V_DIM), lambda b, h: (b, h)),
        out_shape=jax.ShapeDtypeStruct((T, D_INNER), BF16),
        scratch_shapes=[pltpu.VMEM((seq // RET_CHUNK, QK_DIM, V_DIM), BF16)],
        compiler_params=pltpu.CompilerParams(
            dimension_semantics=("arbitrary", "arbitrary"), vmem_limit_bytes=VMEM_LIMIT_BYTES),
        name="retention",
    )(proj, proj, proj, proj, bcast(decay_f), bcast(decay_b), subln_g.reshape(1, V_DIM))


def _out_proj_kernel(o_ref, w_ref, x_ref, g_ref, y_ref, *, final_norm):
    y = x_ref[...] + jnp.dot(o_ref[...], w_ref[...], preferred_element_type=F32)
    if final_norm:
        ms = jnp.mean(y * y, axis=-1, keepdims=True)
        y = y * lax.rsqrt(ms + NORM_EPS) * g_ref[...]
    y_ref[...] = y


def _out_proj(o, w, x2d, g, *, final_norm):
    T, D = x2d.shape
    tm = min(OUT_TM, T)
    assert T % tm == 0
    kern = functools.partial(_out_proj_kernel, final_norm=final_norm)
    return pl.pallas_call(
        kern,
        grid=(T // tm,),
        in_specs=[pl.BlockSpec((tm, D_INNER), lambda i: (i, 0)),
                  pl.BlockSpec((D_INNER, D), lambda i: (0, 0)),
                  pl.BlockSpec((tm, D), lambda i: (i, 0)),
                  pl.BlockSpec((1, D), lambda i: (0, 0))],
        out_specs=pl.BlockSpec((tm, D), lambda i: (i, 0)),
        out_shape=jax.ShapeDtypeStruct((T, D), F32),
        compiler_params=pltpu.CompilerParams(
            dimension_semantics=("arbitrary",), vmem_limit_bytes=VMEM_LIMIT_BYTES),
        name="out_proj",
    )(o, w, x2d, g.reshape(1, D))


def _rot_tables(seq, inv_freq, scale):
    ang = jnp.arange(seq, dtype=F32)[:, None] * inv_freq[None, :]
    cos, sin = jnp.cos(ang), jnp.sin(ang)
    return (jnp.concatenate([cos, cos], axis=1) * scale,
            jnp.concatenate([-sin, sin], axis=1) * scale)


def _trunk(x, p, *, depth):
    batch, seq, _ = x.shape
    x2d = x.reshape(batch * seq, D_MODEL)
    half = QK_DIM // 2
    da_inv = ROPE_THETA ** (-jnp.arange(0, QK_DIM, 2, dtype=F32) / QK_DIM)
    ret_inv = 1.0 / (RET_ROT_BASE ** jnp.linspace(0.0, 1.0, half, dtype=F32))
    qk_scale = QK_DIM ** -0.5
    da_tabs = _rot_tables(seq, da_inv, qk_scale * math.log2(math.e)) + _rot_tables(seq, da_inv, 1.0)
    ret_tabs = _rot_tables(seq, ret_inv, 1.0) + _rot_tables(seq, ret_inv, qk_scale)
    for i in range(depth):
        j = i // 2
        last = i == depth - 1
        if i % 2 == 0:
            lambda_init = 0.8 - 0.6 * math.exp(-0.3 * i)
            proj = _in_proj(x2d, p["norm_g"][i], p["da_w_in"][j], da_tabs, seq=seq,
                            q_width=2 * N_HEADS * QK_DIM, k_width=2 * N_HEADS * QK_DIM,
                            v_width=D_INNER)
            lamv = jnp.stack([p["da_lambda_q1"][j], p["da_lambda_k1"][j],
                              p["da_lambda_q2"][j], p["da_lambda_k2"][j]]).astype(F32)
            o = _diff_attn(proj, lamv, p["da_subln_g"][j], batch=batch, seq=seq,
                           lambda_init=lambda_init)
            w_out = p["da_w_out"][j]
        else:
            proj = _in_proj(x2d, p["norm_g"][i], p["ret_w_in"][j], ret_tabs, seq=seq,
                            q_width=N_HEADS * QK_DIM, k_width=N_HEADS * QK_DIM, v_width=D_INNER)
            o = _retention(proj, p["ret_decay_fwd"][j], p["ret_decay_bwd"][j],
                           p["ret_subln_g"][j], batch=batch, seq=seq)
            w_out = p["ret_w_out"][j]
        x2d = _out_proj(o, w_out, x2d, p["final_norm_g"], final_norm=last)
    return x2d.reshape(batch, seq, D_MODEL)


def _prep_params(norm_g, da_w_in, da_lambda_q1, da_lambda_k1, da_lambda_q2, da_lambda_k2,
                 da_subln_g, da_w_out, ret_w_in, ret_decay_fwd, ret_decay_bwd, ret_subln_g,
                 ret_w_out, final_norm_g):
    perm_head = jnp.concatenate([jnp.arange(0, QK_DIM, 2), jnp.arange(1, QK_DIM, 2)])
    qk_cols = 2 * N_HEADS * QK_DIM
    perm = (jnp.arange(qk_cols // QK_DIM)[:, None] * QK_DIM + perm_head[None, :]).reshape(-1)
    ret_w = jnp.concatenate([ret_w_in[:, :, :qk_cols][:, :, perm], ret_w_in[:, :, qk_cols:]], axis=2)
    return dict(
        norm_g=norm_g.astype(F32), final_norm_g=final_norm_g.astype(F32),
        da_w_in=da_w_in.astype(BF16), da_w_out=da_w_out.astype(BF16),
        da_lambda_q1=da_lambda_q1, da_lambda_k1=da_lambda_k1,
        da_lambda_q2=da_lambda_q2, da_lambda_k2=da_lambda_k2,
        da_subln_g=da_subln_g.astype(F32),
        ret_w_in=ret_w.astype(BF16), ret_w_out=ret_w_out.astype(BF16),
        ret_decay_fwd=ret_decay_fwd, ret_decay_bwd=ret_decay_bwd,
        ret_subln_g=ret_subln_g.astype(F32),
    )


def kernel(x_prompt, x_sample, norm_g, da_w_in, da_lambda_q1, da_lambda_k1, da_lambda_q2,
           da_lambda_k2, da_subln_g, da_w_out, ret_w_in, ret_decay_fwd, ret_decay_bwd,
           ret_subln_g, ret_w_out, final_norm_g):
    p = _prep_params(norm_g, da_w_in, da_lambda_q1, da_lambda_k1, da_lambda_q2, da_lambda_k2,
                     da_subln_g, da_w_out, ret_w_in, ret_decay_fwd, ret_decay_bwd, ret_subln_g,
                     ret_w_out, final_norm_g)
    depth = norm_g.shape[0]
    return (_trunk(x_prompt, p, depth=depth), _trunk(x_sample, p, depth=depth))
```

```python
import functools
import math

import jax
import jax.numpy as jnp
from jax import lax
from jax.experimental import pallas as pl
from jax.experimental.pallas import tpu as pltpu

F32 = jnp.float32
BF16 = jnp.bfloat16

D_MODEL = 1024
D_INNER = 2048
N_HEADS = 8
QK_DIM = 128
V_DIM = 256
ROPE_THETA = 10000.0
RET_ROT_BASE = 10000.0
NORM_EPS = 1e-6
SUBLN_EPS = 1e-5
RET_CHUNK = 512

VMEM_LIMIT_BYTES = 56 * 1024 * 1024

PROJ_TM = 1024
PROJ_ROWS = 256
OUT_TM = 512
ATTN_TQ = 256
ATTN_CK = 512


def _nt_dot(a, b):
    return lax.dot_general(a, b, (((1,), (1,)), ((), ())), preferred_element_type=F32)


def _tn_dot(a, b):
    return lax.dot_general(a, b, (((0,), (0,)), ((), ())), preferred_element_type=F32)


def _in_proj_kernel(x_ref, g_ref, w_ref, cq_ref, sq_ref, ck_ref, sk_ref,
                    q_ref, k_ref, v_ref, gate_ref, h_scr, *, wq, wk, rows):
    @pl.when(pl.program_id(1) == 0)
    def _():
        x = x_ref[...]
        ms = jnp.mean(x * x, axis=-1, keepdims=True)
        h_scr[...] = (x * lax.rsqrt(ms + NORM_EPS) * g_ref[...]).astype(BF16)

    def rotate(x, cos, sin):
        heads = [x[:, t * QK_DIM:(t + 1) * QK_DIM] for t in range(x.shape[1] // QK_DIM)]
        heads = [xh * cos + pltpu.roll(xh, QK_DIM // 2, 1) * sin for xh in heads]
        return heads[0] if len(heads) == 1 else jnp.concatenate(heads, axis=1)

    tm = h_scr.shape[0]
    for r in range(tm // rows):
        rs = slice(r * rows, (r + 1) * rows)
        res = jnp.dot(h_scr[rs, :], w_ref[...], preferred_element_type=F32)
        q_ref[0, rs, :] = rotate(res[:, :wq], cq_ref[rs, :], sq_ref[rs, :]).astype(BF16)
        k_ref[0, rs, :] = rotate(res[:, wq:wq + wk], ck_ref[rs, :], sk_ref[rs, :]).astype(BF16)
        gate = res[:, wq + wk:wq + wk + V_DIM]
        gate_ref[0, rs, :] = (gate * jax.nn.sigmoid(gate)).astype(BF16)
        v_ref[0, rs, :] = res[:, wq + wk + V_DIM:].astype(BF16)


def _in_proj(x2d, g, w, tabs, *, seq, wq, wk):
    T, D = x2d.shape
    tn = wq + wk + 2 * V_DIM
    assert w.shape[1] == N_HEADS * tn
    tm = min(PROJ_TM, seq)
    rows = min(PROJ_ROWS, tm)
    assert T % tm == 0 and seq % tm == 0 and tm % rows == 0
    spb = seq // tm
    tab_spec = pl.BlockSpec((tm, QK_DIM), lambda i, j: (i % spb, 0))
    out_spec = lambda width: pl.BlockSpec((1, tm, width), lambda i, j: (j, i, 0))
    out_shape = lambda width: jax.ShapeDtypeStruct((N_HEADS, T, width), BF16)
    kern = functools.partial(_in_proj_kernel, wq=wq, wk=wk, rows=rows)
    return pl.pallas_call(
        kern,
        grid=(T // tm, N_HEADS),
        in_specs=[
            pl.BlockSpec((tm, D), lambda i, j: (i, 0)),
            pl.BlockSpec((1, D), lambda i, j: (0, 0)),
            pl.BlockSpec((D, tn), lambda i, j: (0, j)),
            tab_spec, tab_spec, tab_spec, tab_spec,
        ],
        out_specs=[out_spec(wq), out_spec(wk), out_spec(V_DIM), out_spec(V_DIM)],
        out_shape=[out_shape(wq), out_shape(wk), out_shape(V_DIM), out_shape(V_DIM)],
        scratch_shapes=[pltpu.VMEM((tm, D), BF16)],
        compiler_params=pltpu.CompilerParams(
            dimension_semantics=("arbitrary", "arbitrary"), vmem_limit_bytes=VMEM_LIMIT_BYTES),
        name="in_proj",
    )(x2d, g.reshape(1, D), w, *tabs)


def _subln_gate(o, g_ref, gate, scale):
    ms = jnp.mean(o * o, axis=-1, keepdims=True)
    o = o * lax.rsqrt(ms + SUBLN_EPS) * g_ref[...]
    if scale != 1.0:
        o = o * scale
    return (o * gate.astype(F32)).astype(BF16)


def _diff_attn_kernel(q_ref, k_ref, v_ref, gate_ref, lamv_ref, g_ref, o_ref,
                      sa_scr, sb_scr, ma_scr, mb_scr, *, tq, ck, lambda_init):
    seq = q_ref.shape[1]
    n_tiles = seq // tq
    lanes = QK_DIM
    buf_a, buf_b = (sa_scr, ma_scr), (sb_scr, mb_scr)
    lv = lamv_ref[...]
    lam = (jnp.exp(jnp.sum(lv[0:1] * lv[1:2], axis=-1, keepdims=True))
           - jnp.exp(jnp.sum(lv[2:3] * lv[3:4], axis=-1, keepdims=True)) + lambda_init)

    def scores(t, bufs):
        s_scr, m_scr = bufs
        r0 = pl.multiple_of(t * tq, tq)
        q = q_ref[0, pl.ds(r0, tq), :]
        for mp in range(2):
            qm = q[:, mp * QK_DIM:(mp + 1) * QK_DIM]
            macc = None
            for kc in range(seq // ck):
                s = _nt_dot(qm, k_ref[0, kc * ck:(kc + 1) * ck, mp * QK_DIM:(mp + 1) * QK_DIM])
                s_scr[mp, :, kc * ck:(kc + 1) * ck] = s
                for g in range(ck // lanes):
                    sg = s[:, g * lanes:(g + 1) * lanes]
                    macc = sg if macc is None else jnp.maximum(macc, sg)
            m_scr[mp] = jnp.broadcast_to(jnp.max(macc, axis=-1, keepdims=True), (tq, lanes))

    def finish(t, bufs):
        s_scr, m_scr = bufs
        r0 = pl.multiple_of(t * tq, tq)
        sums = []
        for mp in range(2):
            m = m_scr[mp]
            lacc = jnp.zeros((tq, lanes), F32)
            for g in range(seq // lanes):
                sl = slice(g * lanes, (g + 1) * lanes)
                p = jnp.exp2(s_scr[mp, :, sl] - m)
                s_scr[mp, :, sl] = p
                lacc = lacc + p
            sums.append(jnp.sum(lacc, axis=-1, keepdims=True))
        r1 = 1.0 / sums[0]
        c = jnp.broadcast_to(lam * sums[0] / sums[1], (tq, lanes))
        acc = jnp.zeros((tq, V_DIM), F32)
        for kc in range(seq // ck):
            parts = []
            for g in range(ck // lanes):
                sl = slice(kc * ck + g * lanes, kc * ck + (g + 1) * lanes)
                parts.append((s_scr[0, :, sl] - c * s_scr[1, :, sl]).astype(BF16))
            a = jnp.concatenate(parts, axis=1)
            acc = acc + jnp.dot(a, v_ref[0, kc * ck:(kc + 1) * ck, :], preferred_element_type=F32)
        o_ref[pl.ds(r0, tq), :] = _subln_gate(acc * r1, g_ref, gate_ref[0, pl.ds(r0, tq), :],
                                              1.0 - lambda_init)

    scores(0, buf_a)

    def body(j, carry):
        t0 = 2 * j
        scores(t0 + 1, buf_b)
        finish(t0, buf_a)
        scores(t0 + 2, buf_a)
        finish(t0 + 1, buf_b)
        return carry

    lax.fori_loop(0, n_tiles // 2 - 1, body, 0)
    scores(n_tiles - 1, buf_b)
    finish(n_tiles - 2, buf_a)
    finish(n_tiles - 1, buf_b)


def _diff_attn(q, k, v, gate, lamv, subln_g, *, batch, seq, lambda_init):
    T = q.shape[1]
    tq, ck = min(ATTN_TQ, seq), min(ATTN_CK, seq)
    assert seq % (2 * tq) == 0 and seq % ck == 0

    blk = pl.BlockSpec((1, seq, V_DIM), lambda b, h: (h, b, 0))
    kern = functools.partial(_diff_attn_kernel, tq=tq, ck=ck, lambda_init=lambda_init)
    return pl.pallas_call(
        kern,
        grid=(batch, N_HEADS),
        in_specs=[blk, blk, blk, blk,
                  pl.BlockSpec((4, QK_DIM), lambda b, h: (0, 0)),
                  pl.BlockSpec((1, V_DIM), lambda b, h: (0, 0))],
        out_specs=pl.BlockSpec((seq, V_DIM), lambda b, h: (b, h)),
        out_shape=jax.ShapeDtypeStruct((T, D_INNER), BF16),
        scratch_shapes=[pltpu.VMEM((2, tq, seq), F32), pltpu.VMEM((2, tq, seq), F32),
                        pltpu.VMEM((2, tq, QK_DIM), F32), pltpu.VMEM((2, tq, QK_DIM), F32)],
        compiler_params=pltpu.CompilerParams(
            dimension_semantics=("arbitrary", "arbitrary"), vmem_limit_bytes=VMEM_LIMIT_BYTES),
        name="diff_attn",
    )(q, k, v, gate, lamv, subln_g.reshape(1, V_DIM))


def _retention_kernel(q_ref, k_ref, v_ref, gate_ref, df_ref, db_ref, g_ref, o_ref,
                      kv_scr, st_scr, *, chunk):
    C = chunk
    seq = q_ref.shape[1]
    n_chunk = seq // C
    lgf = -jnp.exp(df_ref[0])
    lgb = -jnp.exp(db_ref[0])
    pos = lax.broadcasted_iota(jnp.int32, (C, QK_DIM), 0).astype(F32)
    qdec = jnp.concatenate([jnp.exp((pos + 1.0) * lgf), jnp.exp((C - pos) * lgb)], axis=1)
    kdec = jnp.concatenate([jnp.exp((C - 1.0 - pos) * lgf), jnp.exp(pos * lgb)], axis=1)
    cdec_f = jnp.exp(C * lgf)
    cdec_b = jnp.exp(C * lgb)
    cdec_f = jnp.concatenate([cdec_f, cdec_f], axis=1)
    cdec_b = jnp.concatenate([cdec_b, cdec_b], axis=1)

    def rows(c):
        return slice(c * C, (c + 1) * C)

    def twice(x):
        return jnp.concatenate([x, x], axis=1).astype(F32)

    for c in range(n_chunk):
        kd = (twice(k_ref[0, rows(c), :]) * kdec).astype(BF16)
        kv_scr[c] = _tn_dot(kd, v_ref[0, rows(c), :])

    state = jnp.zeros((QK_DIM, V_DIM), F32)
    for c in range(n_chunk):
        st_scr[c, :QK_DIM, :] = state.astype(BF16)
        state = state * cdec_f + kv_scr[c, :QK_DIM, :]
    state = jnp.zeros((QK_DIM, V_DIM), F32)
    for c in reversed(range(n_chunk)):
        st_scr[c, QK_DIM:, :] = state.astype(BF16)
        state = state * cdec_b + kv_scr[c, QK_DIM:, :]

    row = lax.broadcasted_iota(jnp.int32, (C, QK_DIM), 0).astype(F32)
    col = lax.broadcasted_iota(jnp.int32, (C, QK_DIM), 1).astype(F32)
    for c in range(n_chunk):
        qc = q_ref[0, rows(c), :]
        vc = v_ref[0, rows(c), :]
        scores = _nt_dot(qc, k_ref[0, rows(c), :])
        masked = []
        for g in range(C // QK_DIM):
            diff = row - (col + float(g * QK_DIM))
            dmask = jnp.where(diff >= 0, jnp.exp(jnp.maximum(diff, 0.0) * lgf),
                              jnp.exp(jnp.maximum(-diff, 0.0) * lgb))
            masked.append((scores[:, g * QK_DIM:(g + 1) * QK_DIM] * dmask).astype(BF16))
        masked = masked[0] if len(masked) == 1 else jnp.concatenate(masked, axis=1)
        qd = (twice(qc) * qdec).astype(BF16)
        o = (jnp.dot(masked, vc, preferred_element_type=F32)
             + jnp.dot(qd, st_scr[c], preferred_element_type=F32))
        o_ref[rows(c), :] = _subln_gate(o, g_ref, gate_ref[0, rows(c), :], 1.0)


def _retention(q, k, v, gate, decay_f, decay_b, subln_g, *, batch, seq):
    T = q.shape[1]
    chunk = min(RET_CHUNK, seq)
    assert seq % chunk == 0

    def blk(width):
        return pl.BlockSpec((1, seq, width), lambda b, h: (h, b, 0))

    dec_spec = pl.BlockSpec((1, 1, QK_DIM), lambda b, h: (h, 0, 0))
    bcast = lambda d: jnp.broadcast_to(d.astype(F32)[:, None, None], (N_HEADS, 1, QK_DIM))
    return pl.pallas_call(
        functools.partial(_retention_kernel, chunk=chunk),
        grid=(batch, N_HEADS),
        in_specs=[blk(QK_DIM), blk(QK_DIM), blk(V_DIM), blk(V_DIM),
                  dec_spec, dec_spec, pl.BlockSpec((1, V_DIM), lambda b, h: (0, 0))],
        out_specs=pl.BlockSpec((seq, V_DIM), lambda b, h: (b, h)),
        out_shape=jax.ShapeDtypeStruct((T, D_INNER), BF16),
        scratch_shapes=[pltpu.VMEM((seq // chunk, 2 * QK_DIM, V_DIM), F32),
                        pltpu.VMEM((seq // chunk, 2 * QK_DIM, V_DIM), BF16)],
        compiler_params=pltpu.CompilerParams(
            dimension_semantics=("arbitrary", "arbitrary"), vmem_limit_bytes=VMEM_LIMIT_BYTES),
        name="retention",
    )(q, k, v, gate, bcast(decay_f), bcast(decay_b), subln_g.reshape(1, V_DIM))


def _out_proj_kernel(o_ref, w_ref, x_ref, g_ref, y_ref, *, final_norm):
    y = x_ref[...] + jnp.dot(o_ref[...], w_ref[...], preferred_element_type=F32)
    if final_norm:
        ms = jnp.mean(y * y, axis=-1, keepdims=True)
        y = y * lax.rsqrt(ms + NORM_EPS) * g_ref[...]
    y_ref[...] = y


def _out_proj(o, w, x2d, g, *, final_norm):
    T, D = x2d.shape
    tm = min(OUT_TM, T)
    assert T % tm == 0
    kern = functools.partial(_out_proj_kernel, final_norm=final_norm)
    return pl.pallas_call(
        kern,
        grid=(T // tm,),
        in_specs=[pl.BlockSpec((tm, D_INNER), lambda i: (i, 0)),
                  pl.BlockSpec((D_INNER, D), lambda i: (0, 0)),
                  pl.BlockSpec((tm, D), lambda i: (i, 0)),
                  pl.BlockSpec((1, D), lambda i: (0, 0))],
        out_specs=pl.BlockSpec((tm, D), lambda i: (i, 0)),
        out_shape=jax.ShapeDtypeStruct((T, D), F32),
        compiler_params=pltpu.CompilerParams(
            dimension_semantics=("arbitrary",), vmem_limit_bytes=VMEM_LIMIT_BYTES),
        name="out_proj",
    )(o, w, x2d, g.reshape(1, D))


def _rot_tables(seq, inv_freq, scale):
    ang = jnp.arange(seq, dtype=F32)[:, None] * inv_freq[None, :]
    cos, sin = jnp.cos(ang), jnp.sin(ang)
    return (jnp.concatenate([cos, cos], axis=1) * scale,
            jnp.concatenate([-sin, sin], axis=1) * scale)


def _trunk(x, p, *, depth):
    batch, seq, _ = x.shape
    x2d = x.reshape(batch * seq, D_MODEL)
    half = QK_DIM // 2
    da_inv = ROPE_THETA ** (-jnp.arange(0, QK_DIM, 2, dtype=F32) / QK_DIM)
    ret_inv = 1.0 / (RET_ROT_BASE ** jnp.linspace(0.0, 1.0, half, dtype=F32))
    qk_scale = QK_DIM ** -0.5
    da_tabs = _rot_tables(seq, da_inv, qk_scale * math.log2(math.e)) + _rot_tables(seq, da_inv, 1.0)
    ret_tabs = _rot_tables(seq, ret_inv, 1.0) + _rot_tables(seq, ret_inv, qk_scale)
    for i in range(depth):
        j = i // 2
        last = i == depth - 1
        if i % 2 == 0:
            lambda_init = 0.8 - 0.6 * math.exp(-0.3 * i)
            qkvg = _in_proj(x2d, p["norm_g"][i], p["da_w_in"][j], da_tabs, seq=seq,
                            wq=2 * QK_DIM, wk=2 * QK_DIM)
            lamv = jnp.stack([p["da_lambda_q1"][j], p["da_lambda_k1"][j],
                              p["da_lambda_q2"][j], p["da_lambda_k2"][j]]).astype(F32)
            o = _diff_attn(*qkvg, lamv, p["da_subln_g"][j], batch=batch, seq=seq,
                           lambda_init=lambda_init)
            w_out = p["da_w_out"][j]
        else:
            qkvg = _in_proj(x2d, p["norm_g"][i], p["ret_w_in"][j], ret_tabs, seq=seq,
                            wq=QK_DIM, wk=QK_DIM)
            o = _retention(*qkvg, p["ret_decay_fwd"][j], p["ret_decay_bwd"][j],
                           p["ret_subln_g"][j], batch=batch, seq=seq)
            w_out = p["ret_w_out"][j]
        x2d = _out_proj(o, w_out, x2d, p["final_norm_g"], final_norm=last)
    return x2d.reshape(batch, seq, D_MODEL)


def _prep_params(norm_g, da_w_in, da_lambda_q1, da_lambda_k1, da_lambda_q2, da_lambda_k2,
                 da_subln_g, da_w_out, ret_w_in, ret_decay_fwd, ret_decay_bwd, ret_subln_g,
                 ret_w_out, final_norm_g):
    perm_head = jnp.concatenate([jnp.arange(0, QK_DIM, 2), jnp.arange(1, QK_DIM, 2)])
    qk_cols = 2 * N_HEADS * QK_DIM
    perm = (jnp.arange(qk_cols // QK_DIM)[:, None] * QK_DIM + perm_head[None, :]).reshape(-1)
    ret_w = jnp.concatenate([ret_w_in[:, :, :qk_cols][:, :, perm], ret_w_in[:, :, qk_cols:]], axis=2)

    def per_head(w, wq):
        lead = w.shape[:-1]
        parts = jnp.split(w, [N_HEADS * wq, 2 * N_HEADS * wq, 2 * N_HEADS * wq + D_INNER], axis=-1)
        q, k, v, gate = [t.reshape(*lead, N_HEADS, -1) for t in parts]
        return jnp.concatenate([q, k, gate, v], axis=-1).reshape(*lead, -1).astype(BF16)

    return dict(
        norm_g=norm_g.astype(F32), final_norm_g=final_norm_g.astype(F32),
        da_w_in=per_head(da_w_in, 2 * QK_DIM), da_w_out=da_w_out.astype(BF16),
        da_lambda_q1=da_lambda_q1, da_lambda_k1=da_lambda_k1,
        da_lambda_q2=da_lambda_q2, da_lambda_k2=da_lambda_k2,
        da_subln_g=da_subln_g.astype(F32),
        ret_w_in=per_head(ret_w, QK_DIM), ret_w_out=ret_w_out.astype(BF16),
        ret_decay_fwd=ret_decay_fwd, ret_decay_bwd=ret_decay_bwd,
        ret_subln_g=ret_subln_g.astype(F32),
    )


def kernel(x_prompt, x_sample, norm_g, da_w_in, da_lambda_q1, da_lambda_k1, da_lambda_q2,
           da_lambda_k2, da_subln_g, da_w_out, ret_w_in, ret_decay_fwd, ret_decay_bwd,
           ret_subln_g, ret_w_out, final_norm_g):
    p = _prep_params(norm_g, da_w_in, da_lambda_q1, da_lambda_k1, da_lambda_q2, da_lambda_k2,
                     da_subln_g, da_w_out, ret_w_in, ret_decay_fwd, ret_decay_bwd, ret_subln_g,
                     ret_w_out, final_norm_g)
    depth = norm_g.shape[0]
    return (_trunk(x_prompt, p, depth=depth), _trunk(x_sample, p, depth=depth))
```

```python
import functools
import math

import jax
import jax.numpy as jnp
from jax import lax
from jax.experimental import pallas as pl
from jax.experimental.pallas import tpu as pltpu

F32 = jnp.float32
BF16 = jnp.bfloat16

D_MODEL = 1024
D_INNER = 2048
N_HEADS = 8
QK_DIM = 128
V_DIM = 256
ROPE_THETA = 10000.0
RET_ROT_BASE = 10000.0
NORM_EPS = 1e-6
SUBLN_EPS = 1e-5
RET_CHUNK = 512

VMEM_LIMIT_BYTES = 56 * 1024 * 1024

PROJ_TM = 2048
PROJ_ROWS = 256
OUT_TM = 512
ATTN_TQ = 256
ATTN_CK = 512


def _nt_dot(a, b):
    return lax.dot_general(a, b, (((1,), (1,)), ((), ())), preferred_element_type=F32)


def _tn_dot(a, b):
    return lax.dot_general(a, b, (((0,), (0,)), ((), ())), preferred_element_type=F32)


def _in_proj_kernel(x_ref, g_ref, w_ref, cq_ref, sq_ref, ck_ref, sk_ref,
                    q_ref, k_ref, v_ref, gate_ref, h_scr, *, wq, wk, rows):
    @pl.when(pl.program_id(1) == 0)
    def _():
        x = x_ref[...]
        ms = jnp.mean(x * x, axis=-1, keepdims=True)
        h_scr[...] = (x * lax.rsqrt(ms + NORM_EPS) * g_ref[...]).astype(BF16)

    def rotate(x, cos, sin):
        heads = [x[:, t * QK_DIM:(t + 1) * QK_DIM] for t in range(x.shape[1] // QK_DIM)]
        heads = [xh * cos + pltpu.roll(xh, QK_DIM // 2, 1) * sin for xh in heads]
        return heads[0] if len(heads) == 1 else jnp.concatenate(heads, axis=1)

    tm = h_scr.shape[0]
    for r in range(tm // rows):
        rs = slice(r * rows, (r + 1) * rows)
        res = jnp.dot(h_scr[rs, :], w_ref[...], preferred_element_type=F32)
        q_ref[0, rs, :] = rotate(res[:, :wq], cq_ref[rs, :], sq_ref[rs, :]).astype(BF16)
        k_ref[0, rs, :] = rotate(res[:, wq:wq + wk], ck_ref[rs, :], sk_ref[rs, :]).astype(BF16)
        gate = res[:, wq + wk:wq + wk + V_DIM]
        gate_ref[0, rs, :] = (gate * jax.nn.sigmoid(gate)).astype(BF16)
        v_ref[0, rs, :] = res[:, wq + wk + V_DIM:].astype(BF16)


def _in_proj(x2d, g, w, tabs, *, seq, wq, wk):
    T, D = x2d.shape
    tn = wq + wk + 2 * V_DIM
    assert w.shape[1] == N_HEADS * tn
    tm = min(PROJ_TM, seq)
    rows = min(PROJ_ROWS, tm)
    assert T % tm == 0 and seq % tm == 0 and tm % rows == 0
    spb = seq // tm
    tab_spec = pl.BlockSpec((tm, QK_DIM), lambda i, j: (i % spb, 0))
    out_spec = lambda width: pl.BlockSpec((1, tm, width), lambda i, j: (j, i, 0))
    out_shape = lambda width: jax.ShapeDtypeStruct((N_HEADS, T, width), BF16)
    kern = functools.partial(_in_proj_kernel, wq=wq, wk=wk, rows=rows)
    return pl.pallas_call(
        kern,
        grid=(T // tm, N_HEADS),
        in_specs=[
            pl.BlockSpec((tm, D), lambda i, j: (i, 0)),
            pl.BlockSpec((1, D), lambda i, j: (0, 0)),
            pl.BlockSpec((D, tn), lambda i, j: (0, j)),
            tab_spec, tab_spec, tab_spec, tab_spec,
        ],
        out_specs=[out_spec(wq), out_spec(wk), out_spec(V_DIM), out_spec(V_DIM)],
        out_shape=[out_shape(wq), out_shape(wk), out_shape(V_DIM), out_shape(V_DIM)],
        scratch_shapes=[pltpu.VMEM((tm, D), BF16)],
        compiler_params=pltpu.CompilerParams(
            dimension_semantics=("arbitrary", "arbitrary"), vmem_limit_bytes=VMEM_LIMIT_BYTES),
        name="in_proj",
    )(x2d, g.reshape(1, D), w, *tabs)


def _subln_gate(o, g_ref, gate, scale):
    ms = jnp.mean(o * o, axis=-1, keepdims=True)
    o = o * lax.rsqrt(ms + SUBLN_EPS) * g_ref[...]
    if scale != 1.0:
        o = o * scale
    return (o * gate.astype(F32)).astype(BF16)


def _diff_attn_kernel(q_ref, k_ref, v_ref, gate_ref, lamv_ref, g_ref, o_ref,
                      sa_scr, sb_scr, ma_scr, mb_scr, *, tq, ck, lambda_init):
    seq = q_ref.shape[1]
    n_tiles = seq // tq
    lanes = QK_DIM
    buf_a, buf_b = (sa_scr, ma_scr), (sb_scr, mb_scr)
    lv = lamv_ref[...]
    lam = (jnp.exp(jnp.sum(lv[0:1] * lv[1:2], axis=-1, keepdims=True))
           - jnp.exp(jnp.sum(lv[2:3] * lv[3:4], axis=-1, keepdims=True)) + lambda_init)

    def scores(t, bufs):
        s_scr, m_scr = bufs
        r0 = pl.multiple_of(t * tq, tq)
        q = q_ref[0, pl.ds(r0, tq), :]
        for mp in range(2):
            qm = q[:, mp * QK_DIM:(mp + 1) * QK_DIM]
            macc = None
            for kc in range(seq // ck):
                s = _nt_dot(qm, k_ref[0, kc * ck:(kc + 1) * ck, mp * QK_DIM:(mp + 1) * QK_DIM])
                s_scr[mp, :, kc * ck:(kc + 1) * ck] = s
                for g in range(ck // lanes):
                    sg = s[:, g * lanes:(g + 1) * lanes]
                    macc = sg if macc is None else jnp.maximum(macc, sg)
            m_scr[mp] = jnp.broadcast_to(jnp.max(macc, axis=-1, keepdims=True), (tq, lanes))

    def finish(t, bufs):
        s_scr, m_scr = bufs
        r0 = pl.multiple_of(t * tq, tq)
        sums = []
        for mp in range(2):
            m = m_scr[mp]
            lacc = jnp.zeros((tq, lanes), F32)
            for g in range(seq // lanes):
                sl = slice(g * lanes, (g + 1) * lanes)
                p = jnp.exp2(s_scr[mp, :, sl] - m)
                s_scr[mp, :, sl] = p
                lacc = lacc + p
            sums.append(jnp.sum(lacc, axis=-1, keepdims=True))
        r1 = 1.0 / sums[0]
        c = jnp.broadcast_to(lam * sums[0] / sums[1], (tq, lanes))
        acc = jnp.zeros((tq, V_DIM), F32)
        for kc in range(seq // ck):
            parts = []
            for g in range(ck // lanes):
                sl = slice(kc * ck + g * lanes, kc * ck + (g + 1) * lanes)
                parts.append((s_scr[0, :, sl] - c * s_scr[1, :, sl]).astype(BF16))
            a = jnp.concatenate(parts, axis=1)
            acc = acc + jnp.dot(a, v_ref[0, kc * ck:(kc + 1) * ck, :], preferred_element_type=F32)
        o_ref[pl.ds(r0, tq), :] = _subln_gate(acc * r1, g_ref, gate_ref[0, pl.ds(r0, tq), :],
                                              1.0 - lambda_init)

    scores(0, buf_a)

    def body(j, carry):
        t0 = 2 * j
        scores(t0 + 1, buf_b)
        finish(t0, buf_a)
        scores(t0 + 2, buf_a)
        finish(t0 + 1, buf_b)
        return carry

    lax.fori_loop(0, n_tiles // 2 - 1, body, 0)
    scores(n_tiles - 1, buf_b)
    finish(n_tiles - 2, buf_a)
    finish(n_tiles - 1, buf_b)


def _diff_attn(q, k, v, gate, lamv, subln_g, *, batch, seq, lambda_init):
    T = q.shape[1]
    tq, ck = min(ATTN_TQ, seq), min(ATTN_CK, seq)
    assert seq % (2 * tq) == 0 and seq % ck == 0

    blk = pl.BlockSpec((1, seq, V_DIM), lambda b, h: (h, b, 0))
    kern = functools.partial(_diff_attn_kernel, tq=tq, ck=ck, lambda_init=lambda_init)
    return pl.pallas_call(
        kern,
        grid=(batch, N_HEADS),
        in_specs=[blk, blk, blk, blk,
                  pl.BlockSpec((4, QK_DIM), lambda b, h: (0, 0)),
                  pl.BlockSpec((1, V_DIM), lambda b, h: (0, 0))],
        out_specs=pl.BlockSpec((seq, V_DIM), lambda b, h: (b, h)),
        out_shape=jax.ShapeDtypeStruct((T, D_INNER), BF16),
        scratch_shapes=[pltpu.VMEM((2, tq, seq), F32), pltpu.VMEM((2, tq, seq), F32),
                        pltpu.VMEM((2, tq, QK_DIM), F32), pltpu.VMEM((2, tq, QK_DIM), F32)],
        compiler_params=pltpu.CompilerParams(
            dimension_semantics=("arbitrary", "arbitrary"), vmem_limit_bytes=VMEM_LIMIT_BYTES),
        name="diff_attn",
    )(q, k, v, gate, lamv, subln_g.reshape(1, V_DIM))


def _retention_kernel(q_ref, k_ref, v_ref, gate_ref, df_ref, db_ref, g_ref, o_ref,
                      kv_scr, st_scr, *, chunk):
    C = chunk
    seq = q_ref.shape[1]
    n_chunk = seq // C
    lgf = -jnp.exp(df_ref[0])
    lgb = -jnp.exp(db_ref[0])
    pos = lax.broadcasted_iota(jnp.int32, (C, QK_DIM), 0).astype(F32)
    qdec = jnp.concatenate([jnp.exp((pos + 1.0) * lgf), jnp.exp((C - pos) * lgb)], axis=1)
    kdec = jnp.concatenate([jnp.exp((C - 1.0 - pos) * lgf), jnp.exp(pos * lgb)], axis=1)
    cdec_f = jnp.exp(C * lgf)
    cdec_b = jnp.exp(C * lgb)
    cdec_f = jnp.concatenate([cdec_f, cdec_f], axis=1)
    cdec_b = jnp.concatenate([cdec_b, cdec_b], axis=1)

    def rows(c):
        return slice(c * C, (c + 1) * C)

    def twice(x):
        return jnp.concatenate([x, x], axis=1).astype(F32)

    for c in range(n_chunk):
        kd = (twice(k_ref[0, rows(c), :]) * kdec).astype(BF16)
        kv_scr[c] = _tn_dot(kd, v_ref[0, rows(c), :])

    state = jnp.zeros((QK_DIM, V_DIM), F32)
    for c in range(n_chunk):
        st_scr[c, :QK_DIM, :] = state.astype(BF16)
        state = state * cdec_f + kv_scr[c, :QK_DIM, :]
    state = jnp.zeros((QK_DIM, V_DIM), F32)
    for c in reversed(range(n_chunk)):
        st_scr[c, QK_DIM:, :] = state.astype(BF16)
        state = state * cdec_b + kv_scr[c, QK_DIM:, :]

    row = lax.broadcasted_iota(jnp.int32, (C, QK_DIM), 0).astype(F32)
    col = lax.broadcasted_iota(jnp.int32, (C, QK_DIM), 1).astype(F32)
    for c in range(n_chunk):
        qc = q_ref[0, rows(c), :]
        vc = v_ref[0, rows(c), :]
        scores = _nt_dot(qc, k_ref[0, rows(c), :])
        masked = []
        for g in range(C // QK_DIM):
            diff = row - (col + float(g * QK_DIM))
            dmask = jnp.where(diff >= 0, jnp.exp(jnp.maximum(diff, 0.0) * lgf),
                              jnp.exp(jnp.maximum(-diff, 0.0) * lgb))
            masked.append((scores[:, g * QK_DIM:(g + 1) * QK_DIM] * dmask).astype(BF16))
        masked = masked[0] if len(masked) == 1 else jnp.concatenate(masked, axis=1)
        qd = (twice(qc) * qdec).astype(BF16)
        o = (jnp.dot(masked, vc, preferred_element_type=F32)
             + jnp.dot(qd, st_scr[c], preferred_element_type=F32))
        o_ref[rows(c), :] = _subln_gate(o, g_ref, gate_ref[0, rows(c), :], 1.0)


def _retention(q, k, v, gate, decay_f, decay_b, subln_g, *, batch, seq):
    T = q.shape[1]
    chunk = min(RET_CHUNK, seq)
    assert seq % chunk == 0

    def blk(width):
        return pl.BlockSpec((1, seq, width), lambda b, h: (h, b, 0))

    dec_spec = pl.BlockSpec((1, 1, QK_DIM), lambda b, h: (h, 0, 0))
    bcast = lambda d: jnp.broadcast_to(d.astype(F32)[:, None, None], (N_HEADS, 1, QK_DIM))
    return pl.pallas_call(
        functools.partial(_retention_kernel, chunk=chunk),
        grid=(batch, N_HEADS),
        in_specs=[blk(QK_DIM), blk(QK_DIM), blk(V_DIM), blk(V_DIM),
                  dec_spec, dec_spec, pl.BlockSpec((1, V_DIM), lambda b, h: (0, 0))],
        out_specs=pl.BlockSpec((seq, V_DIM), lambda b, h: (b, h)),
        out_shape=jax.ShapeDtypeStruct((T, D_INNER), BF16),
        scratch_shapes=[pltpu.VMEM((seq // chunk, 2 * QK_DIM, V_DIM), F32),
                        pltpu.VMEM((seq // chunk, 2 * QK_DIM, V_DIM), BF16)],
        compiler_params=pltpu.CompilerParams(
            dimension_semantics=("arbitrary", "arbitrary"), vmem_limit_bytes=VMEM_LIMIT_BYTES),
        name="retention",
    )(q, k, v, gate, bcast(decay_f), bcast(decay_b), subln_g.reshape(1, V_DIM))


def _out_proj_kernel(o_ref, w_ref, x_ref, g_ref, y_ref, *, final_norm):
    y = x_ref[...] + jnp.dot(o_ref[...], w_ref[...], preferred_element_type=F32)
    if final_norm:
        ms = jnp.mean(y * y, axis=-1, keepdims=True)
        y = y * lax.rsqrt(ms + NORM_EPS) * g_ref[...]
    y_ref[...] = y


def _out_proj(o, w, x2d, g, *, final_norm):
    T, D = x2d.shape
    tm = min(OUT_TM, T)
    assert T % tm == 0
    kern = functools.partial(_out_proj_kernel, final_norm=final_norm)
    return pl.pallas_call(
        kern,
        grid=(T // tm,),
        in_specs=[pl.BlockSpec((tm, D_INNER), lambda i: (i, 0)),
                  pl.BlockSpec((D_INNER, D), lambda i: (0, 0)),
                  pl.BlockSpec((tm, D), lambda i: (i, 0)),
                  pl.BlockSpec((1, D), lambda i: (0, 0))],
        out_specs=pl.BlockSpec((tm, D), lambda i: (i, 0)),
        out_shape=jax.ShapeDtypeStruct((T, D), F32),
        compiler_params=pltpu.CompilerParams(
            dimension_semantics=("arbitrary",), vmem_limit_bytes=VMEM_LIMIT_BYTES),
        name="out_proj",
    )(o, w, x2d, g.reshape(1, D))


def _rot_tables(seq, inv_freq, scale):
    ang = jnp.arange(seq, dtype=F32)[:, None] * inv_freq[None, :]
    cos, sin = jnp.cos(ang), jnp.sin(ang)
    return (jnp.concatenate([cos, cos], axis=1) * scale,
            jnp.concatenate([-sin, sin], axis=1) * scale)


def _trunk(x, p, *, depth):
    batch, seq, _ = x.shape
    x2d = x.reshape(batch * seq, D_MODEL)
    half = QK_DIM // 2
    da_inv = ROPE_THETA ** (-jnp.arange(0, QK_DIM, 2, dtype=F32) / QK_DIM)
    ret_inv = 1.0 / (RET_ROT_BASE ** jnp.linspace(0.0, 1.0, half, dtype=F32))
    qk_scale = QK_DIM ** -0.5
    da_tabs = _rot_tables(seq, da_inv, qk_scale * math.log2(math.e)) + _rot_tables(seq, da_inv, 1.0)
    ret_tabs = _rot_tables(seq, ret_inv, 1.0) + _rot_tables(seq, ret_inv, qk_scale)
    for i in range(depth):
        j = i // 2
        last = i == depth - 1
        if i % 2 == 0:
            lambda_init = 0.8 - 0.6 * math.exp(-0.3 * i)
            qkvg = _in_proj(x2d, p["norm_g"][i], p["da_w_in"][j], da_tabs, seq=seq,
                            wq=2 * QK_DIM, wk=2 * QK_DIM)
            lamv = jnp.stack([p["da_lambda_q1"][j], p["da_lambda_k1"][j],
                              p["da_lambda_q2"][j], p["da_lambda_k2"][j]]).astype(F32)
            o = _diff_attn(*qkvg, lamv, p["da_subln_g"][j], batch=batch, seq=seq,
                           lambda_init=lambda_init)
            w_out = p["da_w_out"][j]
        else:
            qkvg = _in_proj(x2d, p["norm_g"][i], p["ret_w_in"][j], ret_tabs, seq=seq,
                            wq=QK_DIM, wk=QK_DIM)
            o = _retention(*qkvg, p["ret_decay_fwd"][j], p["ret_decay_bwd"][j],
                           p["ret_subln_g"][j], batch=batch, seq=seq)
            w_out = p["ret_w_out"][j]
        x2d = _out_proj(o, w_out, x2d, p["final_norm_g"], final_norm=last)
    return x2d.reshape(batch, seq, D_MODEL)


def _prep_params(norm_g, da_w_in, da_lambda_q1, da_lambda_k1, da_lambda_q2, da_lambda_k2,
                 da_subln_g, da_w_out, ret_w_in, ret_decay_fwd, ret_decay_bwd, ret_subln_g,
                 ret_w_out, final_norm_g):
    perm_head = jnp.concatenate([jnp.arange(0, QK_DIM, 2), jnp.arange(1, QK_DIM, 2)])
    qk_cols = 2 * N_HEADS * QK_DIM
    perm = (jnp.arange(qk_cols // QK_DIM)[:, None] * QK_DIM + perm_head[None, :]).reshape(-1)
    ret_w = jnp.concatenate([ret_w_in[:, :, :qk_cols][:, :, perm], ret_w_in[:, :, qk_cols:]], axis=2)

    def per_head(w, wq):
        lead = w.shape[:-1]
        parts = jnp.split(w, [N_HEADS * wq, 2 * N_HEADS * wq, 2 * N_HEADS * wq + D_INNER], axis=-1)
        q, k, v, gate = [t.reshape(*lead, N_HEADS, -1) for t in parts]
        return jnp.concatenate([q, k, gate, v], axis=-1).reshape(*lead, -1).astype(BF16)

    return dict(
        norm_g=norm_g.astype(F32), final_norm_g=final_norm_g.astype(F32),
        da_w_in=per_head(da_w_in, 2 * QK_DIM), da_w_out=da_w_out.astype(BF16),
        da_lambda_q1=da_lambda_q1, da_lambda_k1=da_lambda_k1,
        da_lambda_q2=da_lambda_q2, da_lambda_k2=da_lambda_k2,
        da_subln_g=da_subln_g.astype(F32),
        ret_w_in=per_head(ret_w, QK_DIM), ret_w_out=ret_w_out.astype(BF16),
        ret_decay_fwd=ret_decay_fwd, ret_decay_bwd=ret_decay_bwd,
        ret_subln_g=ret_subln_g.astype(F32),
    )


def kernel(x_prompt, x_sample, norm_g, da_w_in, da_lambda_q1, da_lambda_k1, da_lambda_q2,
           da_lambda_k2, da_subln_g, da_w_out, ret_w_in, ret_decay_fwd, ret_decay_bwd,
           ret_subln_g, ret_w_out, final_norm_g):
    p = _prep_params(norm_g, da_w_in, da_lambda_q1, da_lambda_k1, da_lambda_q2, da_lambda_k2,
                     da_subln_g, da_w_out, ret_w_in, ret_decay_fwd, ret_decay_bwd, ret_subln_g,
                     ret_w_out, final_norm_g)
    depth = norm_g.shape[0]
    return (_trunk(x_prompt, p, depth=depth), _trunk(x_sample, p, depth=depth))
```
